```python
import math
import jax
import jax.numpy as jnp
from jax import lax
import numpy as np

D_MODEL = 4096
BATCH = 2
SEQ = 8192
DEPTH = 2

A_HEADS = 16
A_DK = 128
A_DV = 128
A_WIDTH = A_HEADS * A_DK
A_CONV = 4
A_CHUNK = 64
B_GROUPS = ((128, 1), (512, 4), (2048, 16))
B_HEADS_PER_GROUP = 8
B_HEAD_DIM = 128
B_HEADS = B_HEADS_PER_GROUP * len(B_GROUPS)
B_QKV_WIDTH = B_HEADS * B_HEAD_DIM
B_OUT_WIDTH = B_HEADS_PER_GROUP * B_HEAD_DIM
B_BLOCK = 128
ALIBI_MAX_BIAS = 8.0
C_CHUNK = 128
C_GROUPS = 16
C_GROUP_CH = 128
C_WIDTH = C_GROUPS * C_GROUP_CH
N_BRANCH = 3
IN_SIZES = (3 * A_WIDTH, A_HEADS, A_HEADS, A_HEADS * A_DV,
            3 * B_QKV_WIDTH, B_OUT_WIDTH,
            C_WIDTH, C_WIDTH, C_WIDTH,
            N_BRANCH * D_MODEL)
IN_WIDTH = sum(IN_SIZES)
NORM_EPS = 1e-6

kernel_name = 'hybrid_gdn_dilated_gmlp_block'


def rmsnorm(x, g):
    xf = x.astype(jnp.float32)
    y = xf * lax.rsqrt(jnp.mean(xf * xf, axis=-1, keepdims=True) + NORM_EPS)
    return (y * g.astype(jnp.float32)).astype(x.dtype)


def layernorm(x, g, b):
    xf = x.astype(jnp.float32)
    mu = jnp.mean(xf, axis=-1, keepdims=True)
    var = jnp.mean(jnp.square(xf - mu), axis=-1, keepdims=True)
    y = (xf - mu) * lax.rsqrt(var + NORM_EPS)
    return (y * g.astype(jnp.float32) + b.astype(jnp.float32)).astype(x.dtype)


def l2norm(x):
    return x * lax.rsqrt(jnp.sum(x * x, axis=-1, keepdims=True) + NORM_EPS)


def alibi_slopes(n):
    return 2.0 ** (-ALIBI_MAX_BIAS * jnp.arange(1, n + 1, dtype=jnp.float32) / n)


def causal_depthwise_conv(x, w):
    k, ch = w.shape
    return lax.conv_general_dilated(x, w.astype(x.dtype)[:, None, :], (1,), ((k - 1, 0),),
                                    dimension_numbers=('NWC', 'WIO', 'NWC'),
                                    feature_group_count=ch)


def chunk_gated_delta_rule(q, k, v, g, beta):
    bsz, seq, heads, dk = q.shape
    dv = v.shape[-1]
    c = A_CHUNK
    n = seq // c

    def chunks(t):
        t = t.reshape((bsz, n, c, heads) + t.shape[3:])
        return jnp.moveaxis(t, 3, 2)

    q, k, v, g, beta = [chunks(t) for t in (q, k, v, g, beta)]
    G = jnp.cumsum(g, axis=-1)
    tril = jnp.tril(jnp.ones((c, c), dtype=bool))
    strict = jnp.tril(jnp.ones((c, c), dtype=bool), -1)
    diff = G[..., :, None] - G[..., None, :]
    decay = jnp.where(tril, jnp.exp(jnp.where(tril, diff, 0.0)), 0.0)
    k_beta = k * beta[..., None]
    a_strict = jnp.where(strict, jnp.einsum('bnhid,bnhjd->bnhij', k_beta, k) * decay, 0.0)
    lower = a_strict + jnp.eye(c, dtype=a_strict.dtype)
    rhs = jnp.concatenate([v * beta[..., None], k_beta * jnp.exp(G)[..., None]], axis=-1)
    wy = lax.linalg.triangular_solve(lower, rhs, left_side=True, lower=True, unit_diagonal=True)
    u, k_cum = wy[..., :dv], wy[..., dv:]
    qk = jnp.where(tril, jnp.einsum('bnhid,bnhjd->bnhij', q, k) * decay, 0.0)
    q_dec = q * jnp.exp(G)[..., None]
    k_dec = k * jnp.exp(G[..., -1:] - G)[..., None]
    chunk_decay = jnp.exp(G[..., -1])

    def step(state, xs):
        u_n, kc_n, qk_n, qd_n, kd_n, cd_n = xs
        v_new = u_n - jnp.einsum('bhcd,bhde->bhce', kc_n, state)
        o_n = jnp.einsum('bhcd,bhde->bhce', qd_n, state) + jnp.einsum('bhij,bhje->bhie', qk_n, v_new)
        state = state * cd_n[..., None, None] + jnp.einsum('bhcd,bhce->bhde', kd_n, v_new)
        return state, o_n

    xs = tuple(jnp.moveaxis(t, 1, 0) for t in (u, k_cum, qk, q_dec, k_dec, chunk_decay))
    state0 = jnp.zeros((bsz, heads, dk, dv), jnp.float32)
    _, o = lax.scan(step, state0, xs)
    return jnp.transpose(o, (1, 0, 3, 2, 4)).reshape(bsz, seq, heads, dv)


def gated_deltanet(qkv, beta_logit, alpha_logit, gate, conv_w, a_log, dt_bias, g_onorm):
    bsz, seq, _ = qkv.shape
    out_dtype = qkv.dtype
    f32 = jnp.float32
    qkv = jax.nn.silu(causal_depthwise_conv(qkv, conv_w)).astype(f32)
    q, k, v = jnp.split(qkv, 3, axis=-1)
    q = l2norm(q.reshape(bsz, seq, A_HEADS, A_DK)) * (A_DK ** -0.5)
    k = l2norm(k.reshape(bsz, seq, A_HEADS, A_DK))
    v = v.reshape(bsz, seq, A_HEADS, A_DV)
    beta = jax.nn.sigmoid(beta_logit.astype(f32))
    g = -jnp.exp(a_log.astype(f32)) * jax.nn.softplus(alpha_logit.astype(f32) + dt_bias.astype(f32))
    o = chunk_gated_delta_rule(q, k, v, g, beta)
    o = rmsnorm(o, g_onorm) * jax.nn.silu(gate.astype(f32).reshape(bsz, seq, A_HEADS, A_DV))
    return o.reshape(bsz, seq, A_HEADS * A_DV).astype(out_dtype)


def dilated_window_group(q, k, v, window, dilation, slopes):
    bsz, seq, heads, hd = q.shape
    f32 = jnp.float32
    n_back = window // dilation
    unit = B_BLOCK * dilation
    s_pad = -(-seq // unit) * unit
    nb = s_pad // unit
    pad = ((0, 0), (0, s_pad - seq), (0, 0), (0, 0))
    blocks = (bsz, nb, B_BLOCK, dilation, heads, hd)
    qb, kb, vb = [jnp.pad(t.astype(f32), pad).reshape(blocks) for t in (q, k, v)]
    prev = ((0, 0), (1, 0), (0, 0), (0, 0), (0, 0), (0, 0))
    kk = jnp.concatenate([jnp.pad(kb[:, :-1], prev), kb], axis=2)
    vv = jnp.concatenate([jnp.pad(vb[:, :-1], prev), vb], axis=2)
    s = jnp.einsum('bnirhd,bnjrhd->bnrhij', qb, kk) * (hd ** -0.5)
    i = jnp.arange(B_BLOCK)[:, None]
    j = jnp.arange(2 * B_BLOCK)[None, :]
    rel = i + B_BLOCK - j
    blk = jnp.arange(nb)[:, None, None]
    valid = (rel >= 0) & (rel <= n_back) & ((blk > 0) | (j >= B_BLOCK))
    bias = -slopes[:, None, None] * (dilation * rel).astype(f32)
    s = jnp.where(valid[None, :, None, None], s + bias, -jnp.inf)
    m = jnp.max(s, axis=-1, keepdims=True)
    p = jnp.exp(s - m)
    den = jnp.sum(p, axis=-1, keepdims=True)
    o = jnp.einsum('bnrhij,bnjrhd->bnirhd', p / den, vv)
    lse = jnp.transpose((m + jnp.log(den))[..., 0], (0, 1, 4, 2, 3))
    o = o.reshape(bsz, s_pad, heads, hd)[:, :seq]
    lse = lse.reshape(bsz, s_pad, heads)[:, :seq]
    return o, lse


def dilated_attention(qkv, gate):
    bsz, seq, _ = qkv.shape
    q, k, v = [t.reshape(bsz, seq, B_HEADS, B_HEAD_DIM) for t in jnp.split(qkv, 3, axis=-1)]
    slopes = alibi_slopes(B_HEADS)
    outs, lses = [], []
    for gi, (window, dilation) in enumerate(B_GROUPS):
        hs = slice(gi * B_HEADS_PER_GROUP, (gi + 1) * B_HEADS_PER_GROUP)
        o, lse = dilated_window_group(q[:, :, hs], k[:, :, hs], v[:, :, hs], window, dilation, slopes[hs])
        outs.append(o)
        lses.append(lse)
    weights = jax.nn.softmax(jnp.stack(lses, axis=0), axis=0)
    o = jnp.sum(weights[..., None] * jnp.stack(outs, axis=0), axis=0)
    o = o.reshape(bsz, seq, B_OUT_WIDTH) * jax.nn.silu(gate.astype(jnp.float32))
    return o.astype(qkv.dtype)


def spatial_gating(u, v, gate, ln_g, ln_b, w_s, b_s):
    bsz, seq, _ = u.shape
    n = seq // C_CHUNK
    u = jax.nn.gelu(u)
    v = layernorm(jax.nn.gelu(v), ln_g, ln_b)
    vc = v.reshape(bsz, n, C_CHUNK, C_GROUPS, C_GROUP_CH)
    causal = jnp.tril(jnp.ones((C_CHUNK, C_CHUNK), dtype=bool))
    w_causal = jnp.where(causal, w_s, 0.0).astype(v.dtype)
    sv = jnp.einsum('gts,bnsgc->bntgc', w_causal, vc) + jnp.transpose(b_s)[:, :, None]
    return u * sv.reshape(bsz, seq, C_WIDTH) * jax.nn.silu(gate)


def hybrid_layer(x, cond, w_ada, b_ada, g_pre, g_post, w_in, conv_a, a_log, dt_bias, g_onorm_a,
                 ln_c_g, ln_c_b, w_spatial, b_spatial, w_br_a, w_br_b, w_br_c, w_out):
    bsz, seq, _ = x.shape
    mod = cond @ w_ada + b_ada
    shift, scale, gate = jnp.split(mod, 3, axis=-1)
    h = rmsnorm(x, g_pre) * (1 + scale[:, None, :]) + shift[:, None, :]
    proj = h @ w_in
    offsets = np.cumsum(IN_SIZES)[:-1].tolist()
    (a_qkv, a_beta, a_alpha, a_gate, b_qkv, b_gate,
     c_u, c_v, c_gate, merge) = jnp.split(proj, offsets, axis=-1)
    y_a = gated_deltanet(a_qkv, a_beta, a_alpha, a_gate, conv_a, a_log, dt_bias, g_onorm_a) @ w_br_a
    y_b = dilated_attention(b_qkv, b_gate) @ w_br_b
    y_c = spatial_gating(c_u, c_v, c_gate, ln_c_g, ln_c_b, w_spatial, b_spatial) @ w_br_c
    gates = jax.nn.sigmoid(merge.astype(jnp.float32)).astype(x.dtype).reshape(bsz, seq, N_BRANCH, D_MODEL)
    y = gates[:, :, 0] * y_a + gates[:, :, 1] * y_b + gates[:, :, 2] * y_c
    out = y @ w_out
    return x + gate[:, None, :] * rmsnorm(out, g_post)


def setup_inputs(seed: int = 0) -> dict:
    key = jax.random.key(seed)
    ks = jax.random.split(key, 19)
    f32 = jnp.float32

    def nrm(k, shape, scale):
        return jax.random.normal(k, shape, f32) * scale

    def near_one(k, shape):
        return 1.0 + nrm(k, shape, 0.02)

    dt = jnp.exp(jax.random.uniform(ks[9], (DEPTH, A_HEADS), f32, math.log(0.001), math.log(0.1)))
    return {
        'x': nrm(ks[0], (BATCH, SEQ, D_MODEL), 1.0),
        'c': nrm(ks[1], (BATCH, D_MODEL), 1.0),
        'w_ada': nrm(ks[2], (DEPTH, D_MODEL, 3 * D_MODEL), 0.1 * D_MODEL ** -0.5),
        'b_ada': nrm(ks[3], (DEPTH, 3 * D_MODEL), 0.02),
        'g_pre': near_one(ks[4], (DEPTH, D_MODEL)),
        'g_post': near_one(ks[5], (DEPTH, D_MODEL)),
        'w_in': nrm(ks[6], (DEPTH, D_MODEL, IN_WIDTH), D_MODEL ** -0.5),
        'conv_a': nrm(ks[7], (DEPTH, A_CONV, 3 * A_WIDTH), A_CONV ** -0.5),
        'a_log': jnp.log(jax.random.uniform(ks[8], (DEPTH, A_HEADS), f32, 1.0, 16.0)),
        'dt_bias': dt + jnp.log(-jnp.expm1(-dt)),
        'g_onorm_a': near_one(ks[10], (DEPTH, A_DV)),
        'ln_c_g': near_one(ks[11], (DEPTH, C_WIDTH)),
        'ln_c_b': nrm(ks[12], (DEPTH, C_WIDTH), 0.02),
        'w_spatial': nrm(ks[13], (DEPTH, C_GROUPS, C_CHUNK, C_CHUNK), C_CHUNK ** -0.5),
        'b_spatial': near_one(ks[14], (DEPTH, C_GROUPS, C_CHUNK)),
        'w_br_a': nrm(ks[15], (DEPTH, A_HEADS * A_DV, D_MODEL), (A_HEADS * A_DV) ** -0.5),
        'w_br_b': nrm(ks[16], (DEPTH, B_OUT_WIDTH, D_MODEL), B_OUT_WIDTH ** -0.5),
        'w_br_c': nrm(ks[17], (DEPTH, C_WIDTH, D_MODEL), C_WIDTH ** -0.5),
        'w_out': nrm(ks[18], (DEPTH, D_MODEL, D_MODEL), D_MODEL ** -0.5),
    }


def reference(x, c, w_ada, b_ada, g_pre, g_post, w_in, conv_a, a_log, dt_bias, g_onorm_a,
              ln_c_g, ln_c_b, w_spatial, b_spatial, w_br_a, w_br_b, w_br_c, w_out):
    cond = jax.nn.silu(c)
    for l in range(DEPTH):
        x = hybrid_layer(x, cond, w_ada[l], b_ada[l], g_pre[l], g_post[l], w_in[l], conv_a[l],
                         a_log[l], dt_bias[l], g_onorm_a[l], ln_c_g[l], ln_c_b[l], w_spatial[l],
                         b_spatial[l], w_br_a[l], w_br_b[l], w_br_c[l], w_out[l])
    return x
```

```python
import functools

import jax
import jax.numpy as jnp
from jax import lax
from jax.experimental import pallas as pl
from jax.experimental.pallas import tpu as pltpu

F32 = jnp.float32
MXU_DTYPE = jnp.bfloat16
NORM_EPS = 1e-6

D_MODEL = 4096
A_HEADS = 16
A_DK = 128
A_CONV = 4
A_CHUNK = 64
B_GROUPS = ((128, 1), (512, 4), (2048, 16))
B_HPG = 8
B_HEADS = 24
B_HD = 128
B_BLOCK = 128
ALIBI_MAX_BIAS = 8.0
C_CHUNK = 128
C_GROUPS = 16
N_BRANCH = 3

COL_A_QKV, COL_A_GATE, COL_B_QKV, COL_B_GATE = 0, 6144, 8192, 17408
COL_C_U, COL_C_V, COL_C_GATE, COL_MERGE = 18432, 20480, 22528, 24576
MAIN_WIDTH = 36864
LANES = 128

VMEM_LIMIT = 56 * 1024 * 1024


def _cparams(sem):
    return pltpu.CompilerParams(dimension_semantics=sem, vmem_limit_bytes=VMEM_LIMIT)


def _mx(x):
    return x.astype(MXU_DTYPE)


def _dot(a, b):
    return jnp.dot(_mx(a), _mx(b), preferred_element_type=F32)


def _dot_nt(a, b):
    return lax.dot_general(_mx(a), _mx(b), (((1,), (1,)), ((), ())), preferred_element_type=F32)


def _dot_tn(a, b):
    return lax.dot_general(_mx(a), _mx(b), (((0,), (0,)), ((), ())), preferred_element_type=F32)


def _dot_f32(a, b):
    return jnp.dot(a, b, preferred_element_type=F32, precision=lax.Precision.HIGHEST)


def _sigmoid(x):
    return 1.0 / (1.0 + jnp.exp(-x))


def _silu(x):
    return x * _sigmoid(x)


def _gelu_tanh(x):
    return 0.5 * x * (1.0 + jnp.tanh(0.7978845608028654 * (x + 0.044715 * (x * x * x))))


def _ada_kernel(c_ref, w_ref, b_ref, o_ref):
    cond = _silu(c_ref[...])
    o_ref[0] = _dot(cond, w_ref[0]) + b_ref[0]


def _ada(c_pad, w_ada, b_ada):
    depth, d, n = w_ada.shape
    tn = 512
    return pl.pallas_call(
        _ada_kernel,
        grid=(depth, n // tn),
        in_specs=[
            pl.BlockSpec((8, d), lambda l, j: (0, 0)),
            pl.BlockSpec((1, d, tn), lambda l, j: (l, 0, j)),
            pl.BlockSpec((1, 1, tn), lambda l, j: (l, 0, j)),
        ],
        out_specs=pl.BlockSpec((1, 8, tn), lambda l, j: (l, 0, j)),
        out_shape=jax.ShapeDtypeStruct((depth, 8, n), F32),
        compiler_params=_cparams(("arbitrary", "arbitrary")),
        name="ada_mod",
    )(c_pad, w_ada, b_ada.reshape(depth, 1, n))


def _prenorm_kernel(x_ref, g_ref, shift_ref, scale_ref, wba_ref, h_ref, ba_ref):
    x = x_ref[0]
    ms = jnp.mean(x * x, axis=-1, keepdims=True)
    y = x * lax.rsqrt(ms + NORM_EPS) * g_ref[...]
    h = y * (1.0 + scale_ref[0]) + shift_ref[0]
    hb = _mx(h)
    h_ref[0] = hb
    ba_ref[0] = jnp.dot(hb, wba_ref[...], preferred_element_type=F32)


def _prenorm(x, g_pre, mod3, w_ba):
    bsz, seq, d = x.shape
    ts = 512
    return pl.pallas_call(
        _prenorm_kernel,
        grid=(bsz, seq // ts),
        in_specs=[
            pl.BlockSpec((1, ts, d), lambda b, t: (b, t, 0)),
            pl.BlockSpec((1, d), lambda b, t: (0, 0)),
            pl.BlockSpec((1, 1, d), lambda b, t: (b, 0, 0)),
            pl.BlockSpec((1, 1, d), lambda b, t: (b, 0, 1)),
            pl.BlockSpec((d, LANES), lambda b, t: (0, 0)),
        ],
        out_specs=[
            pl.BlockSpec((1, ts, d), lambda b, t: (b, t, 0)),
            pl.BlockSpec((1, ts, LANES), lambda b, t: (b, t, 0)),
        ],
        out_shape=[
            jax.ShapeDtypeStruct((bsz, seq, d), MXU_DTYPE),
            jax.ShapeDtypeStruct((bsz, seq, LANES), F32),
        ],
        compiler_params=_cparams(("arbitrary", "arbitrary")),
        name="prenorm",
    )(x, g_pre.reshape(1, d), mod3, mod3, w_ba)


def _matmul_kernel(a_ref, b_ref, o_ref):
    o_ref[...] = jnp.dot(a_ref[...], b_ref[...], preferred_element_type=F32).astype(o_ref.dtype)


def _matmul(a, b, out_dtype, tm, tn, name):
    m, k = a.shape
    _, n = b.shape
    return pl.pallas_call(
        _matmul_kernel,
        grid=(m // tm, n // tn),
        in_specs=[
            pl.BlockSpec((tm, k), lambda i, j: (i, 0)),
            pl.BlockSpec((k, tn), lambda i, j: (0, j)),
        ],
        out_specs=pl.BlockSpec((tm, tn), lambda i, j: (i, j)),
        out_shape=jax.ShapeDtypeStruct((m, n), out_dtype),
        compiler_params=_cparams(("arbitrary", "arbitrary")),
        name=name,
    )(a, b)


GDN_T = 256
GDN_NC = GDN_T // A_CHUNK


def _block_masks(n):
    i = lax.broadcasted_iota(jnp.int32, (n, n), 0)
    j = lax.broadcasted_iota(jnp.int32, (n, n), 1)
    lower = i >= j
    strict = i > j
    same = lambda s: (i >> (s.bit_length() - 1)) == (j >> (s.bit_length() - 1))
    return i, j, lower, strict, same


def _gdn_kernel(q_ref, k_ref, v_ref, gate_ref, ba_ref, cw_ref, hp_ref, gon_ref, o_ref,
                state_ref, hist_ref):
    h = pl.program_id(1)
    t = pl.program_id(2)
    T = GDN_T

    @pl.when(t == 0)
    def _():
        state_ref[...] = jnp.zeros_like(state_ref)
        hist_ref[:, 0:8, :] = jnp.zeros((3, 8, LANES), F32)

    def conv(p, x_ref):
        hist_ref[p, 8:, :] = x_ref[0]
        w = cw_ref[0, p]
        acc = hist_ref[p, pl.ds(8, T), :] * w[3:4, :]
        acc += hist_ref[p, pl.ds(7, T), :] * w[2:3, :]
        acc += hist_ref[p, pl.ds(6, T), :] * w[1:2, :]
        acc += hist_ref[p, pl.ds(5, T), :] * w[0:1, :]
        hist_ref[p, 0:8, :] = hist_ref[p, T:T + 8, :]
        return _silu(acc)

    q = conv(0, q_ref)
    k = conv(1, k_ref)
    v = conv(2, v_ref)
    q = q * lax.rsqrt(jnp.sum(q * q, axis=-1, keepdims=True) + NORM_EPS) * (A_DK ** -0.5)
    k = k * lax.rsqrt(jnp.sum(k * k, axis=-1, keepdims=True) + NORM_EPS)

    ba = ba_ref[0]
    lane = lax.broadcasted_iota(jnp.int32, (T, LANES), 1)
    bl = jnp.sum(jnp.where(lane == h, ba, 0.0), axis=-1, keepdims=True)
    al = jnp.sum(jnp.where(lane == h + A_HEADS, ba, 0.0), axis=-1, keepdims=True)
    beta = jnp.broadcast_to(_sigmoid(bl), (T, LANES))
    a_log = hp_ref[0, 0:1, :]
    dt_b = hp_ref[0, 1:2, :]
    z = jnp.broadcast_to(al, (T, LANES)) + dt_b
    softplus = jnp.maximum(z, 0.0) + jnp.log(1.0 + jnp.exp(-jnp.abs(z)))
    g = -jnp.exp(a_log) * softplus

    _, _, lower, strict, same = _block_masks(T)
    in_chunk = same(A_CHUNK)
    tri = jnp.where(lower & in_chunk, 1.0, 0.0)
    ones_bd = jnp.where(in_chunk, 1.0, 0.0)
    gc = _dot_f32(tri, g)
    glast = _dot_f32(ones_bd, g)
    gc2 = jnp.concatenate([gc, gc], axis=1)
    diff = gc2 - gc2.T
    lo_mask = lower & in_chunk
    decay = jnp.where(lo_mask, jnp.exp(jnp.where(lo_mask, diff, 0.0)), 0.0)

    kb = k * beta
    a_mat = jnp.where(strict & in_chunk, _dot_nt(kb, k) * decay, 0.0)
    qk = jnp.where(lo_mask, _dot_nt(q, k) * decay, 0.0)
    eg = jnp.exp(gc)
    rhs = jnp.concatenate([v * beta, kb * eg], axis=1)

    s16, s32 = same(16), same(32)
    eye = jnp.where(lower & ~strict, 1.0, 0.0)
    d = jnp.where(s16, a_mat, 0.0)
    d2 = _dot(d, d)
    d4 = _dot(d2, d2)
    d8 = _dot(d4, d4)
    p = eye - d
    p = p + _dot(p, d2)
    p = p + _dot(p, d4)
    p = p + _dot(p, d8)
    e1 = jnp.where(s32 & ~s16, a_mat, 0.0)
    p = p - _dot(_dot(p, e1), p)
    e2 = jnp.where(in_chunk & ~s32, a_mat, 0.0)
    p = p - _dot(_dot(p, e2), p)
    wy = _dot(p, rhs)
    u = wy[:, :LANES]
    k_cum = wy[:, LANES:]

    q_dec = q * eg
    k_dec = k * jnp.exp(glast - gc)
    cd = jnp.exp(glast)

    gon = gon_ref[...]
    gate = gate_ref[0]
    state = state_ref[...]
    for c in range(GDN_NC):
        r = slice(c * A_CHUNK, (c + 1) * A_CHUNK)
        kq = jnp.concatenate([k_cum[r], q_dec[r]], axis=0)
        res = _dot(kq, state)
        v_new = u[r] - res[:A_CHUNK]
        o_c = res[A_CHUNK:] + _dot(qk[r, c * A_CHUNK:(c + 1) * A_CHUNK], v_new)
        state = state * cd[c * A_CHUNK:c * A_CHUNK + 1, :] + _dot_tn(k_dec[r], v_new)
        ms = jnp.mean(o_c * o_c, axis=-1, keepdims=True)
        o_n = o_c * lax.rsqrt(ms + NORM_EPS) * gon
        o_ref[0, r, :] = (o_n * _silu(gate[r])).astype(o_ref.dtype)
    state_ref[...] = state


def _gdn(proj3, ba3, cw, hp, gon):
    bsz, seq, _ = proj3.shape
    T = GDN_T
    qb, kb_, vb, gb = (COL_A_QKV // LANES, COL_A_QKV // LANES + A_HEADS,
                       COL_A_QKV // LANES + 2 * A_HEADS, COL_A_GATE // LANES)
    return pl.pallas_call(
        _gdn_kernel,
        grid=(bsz, A_HEADS, seq // T),
        in_specs=[
            pl.BlockSpec((1, T, LANES), lambda b, h, t: (b, t, qb + h)),
            pl.BlockSpec((1, T, LANES), lambda b, h, t: (b, t, kb_ + h)),
            pl.BlockSpec((1, T, LANES), lambda b, h, t: (b, t, vb + h)),
            pl.BlockSpec((1, T, LANES), lambda b, h, t: (b, t, gb + h)),
            pl.BlockSpec((1, T, LANES), lambda b, h, t: (b, t, 0)),
            pl.BlockSpec((1, 3, 8, LANES), lambda b, h, t: (h, 0, 0, 0)),
            pl.BlockSpec((1, 8, LANES), lambda b, h, t: (h, 0, 0)),
            pl.BlockSpec((1, LANES), lambda b, h, t: (0, 0)),
        ],
        out_specs=pl.BlockSpec((1, T, LANES), lambda b, h, t: (b, t, h)),
        out_shape=jax.ShapeDtypeStruct((bsz, seq, A_HEADS * A_DK), MXU_DTYPE),
        scratch_shapes=[
            pltpu.VMEM((A_DK, A_DK), F32),
            pltpu.VMEM((3, T + 8, LANES), F32),
        ],
        compiler_params=_cparams(("arbitrary", "arbitrary", "arbitrary")),
        name="gdn",
    )(proj3, proj3, proj3, proj3, ba3, cw, hp, gon)


def _attn_kernel(q_ref, kp_ref, kc_ref, vp_ref, vc_ref, o_ref, lse_ref, *, dilation, group):
    n = pl.program_id(2)
    blk = B_BLOCK
    i = lax.broadcasted_iota(jnp.int32, (blk, 2 * blk), 0)
    j = lax.broadcasted_iota(jnp.int32, (blk, 2 * blk), 1)
    rel = i + blk - j
    valid = (rel >= 0) & (rel <= blk) & jnp.logical_or(j >= blk, n > 0)
    relf = rel.astype(F32)
    neg = jnp.where(valid, 0.0, -jnp.inf)
    scale = B_HD ** -0.5
    for hh in range(B_HPG):
        head = group * B_HPG + hh
        slope = 2.0 ** (-ALIBI_MAX_BIAS * (head + 1) / B_HEADS)
        sl = slice(hh * B_HD, (hh + 1) * B_HD)
        kk = jnp.concatenate([kp_ref[0, :, sl], kc_ref[0, :, sl]], axis=0)
        vv = jnp.concatenate([vp_ref[0, :, sl], vc_ref[0, :, sl]], axis=0)
        s = _dot_nt(q_ref[0, :, sl], kk) * scale + relf * (-slope * dilation) + neg
        m = jnp.max(s, axis=-1, keepdims=True)
        p = jnp.exp(s - m)
        l = jnp.sum(p, axis=-1, keepdims=True)
        acc = _dot(p, vv)
        o_ref[0, :, sl] = acc / l
        lse_ref[0, :, sl] = jnp.broadcast_to(m + jnp.log(l), (blk, B_HD))


def _attn_group(proj3, group, dilation):
    bsz, seq, width = proj3.shape
    rows = seq // dilation
    nb = rows // B_BLOCK
    wblk = B_HPG * B_HD
    per_row = width // wblk
    view = proj3.reshape(bsz, rows, dilation * width)
    qc = COL_B_QKV // wblk + group
    kc = qc + B_HEADS // B_HPG
    vc = kc + B_HEADS // B_HPG
    cur = lambda c: (lambda b, r, n: (b, n, r * per_row + c))
    prev = lambda c: (lambda b, r, n: (b, jnp.maximum(n - 1, 0), r * per_row + c))
    blk = (1, B_BLOCK, wblk)
    o, lse = pl.pallas_call(
        functools.partial(_attn_kernel, dilation=dilation, group=group),
        grid=(bsz, dilation, nb),
        in_specs=[
            pl.BlockSpec(blk, cur(qc)),
            pl.BlockSpec(blk, prev(kc)),
            pl.BlockSpec(blk, cur(kc)),
            pl.BlockSpec(blk, prev(vc)),
            pl.BlockSpec(blk, cur(vc)),
        ],
        out_specs=[
            pl.BlockSpec(blk, lambda b, r, n: (b, n, r)),
            pl.BlockSpec(blk, lambda b, r, n: (b, n, r)),
        ],
        out_shape=[
            jax.ShapeDtypeStruct((bsz, rows, dilation * wblk), F32),
            jax.ShapeDtypeStruct((bsz, rows, dilation * wblk), F32),
        ],
        compiler_params=_cparams(("arbitrary", "arbitrary", "arbitrary")),
        name=f"attn_g{group}",
    )(view, view, view, view, view)
    return o.reshape(bsz, seq, wblk), lse.reshape(bsz, seq, wblk)


def _attn_combine_kernel(o0, l0, o1, l1, o2, l2, gate_ref, out_ref):
    a0, a1, a2 = l0[0], l1[0], l2[0]
    m = jnp.maximum(jnp.maximum(a0, a1), a2)
    w0, w1, w2 = jnp.exp(a0 - m), jnp.exp(a1 - m), jnp.exp(a2 - m)
    o = (w0 * o0[0] + w1 * o1[0] + w2 * o2[0]) / (w0 + w1 + w2)
    out_ref[0] = (o * _silu(gate_ref[0])).astype(out_ref.dtype)


def _attn_combine(parts, proj3):
    bsz, seq, _ = proj3.shape
    wblk = B_HPG * B_HD
    ts = 512
    spec = pl.BlockSpec((1, ts, wblk), lambda b, t: (b, t, 0))
    return pl.pallas_call(
        _attn_combine_kernel,
        grid=(bsz, seq // ts),
        in_specs=[spec] * 6 + [pl.BlockSpec((1, ts, wblk), lambda b, t: (b, t, COL_B_GATE // wblk))],
        out_specs=spec,
        out_shape=jax.ShapeDtypeStruct((bsz, seq, wblk), MXU_DTYPE),
        compiler_params=_cparams(("arbitrary", "arbitrary")),
        name="attn_combine",
    )(*parts, proj3)


GMLP_T = 256


def _gmlp_kernel(u_ref, v_ref, gate_ref, lng_ref, lnb_ref, ws_ref, bias_ref, o_ref):
    width = C_GROUPS * LANES
    vv = _gelu_tanh(v_ref[...])
    mu = jnp.mean(vv, axis=-1, keepdims=True)
    xc = vv - mu
    var = jnp.mean(xc * xc, axis=-1, keepdims=True)
    vn = xc * lax.rsqrt(var + NORM_EPS) * lng_ref[...] + lnb_ref[...]
    ti = lax.broadcasted_iota(jnp.int32, (C_CHUNK, C_CHUNK), 0)
    si = lax.broadcasted_iota(jnp.int32, (C_CHUNK, C_CHUNK), 1)
    causal = ti >= si
    for g in range(C_GROUPS):
        w = jnp.where(causal, ws_ref[g], 0.0)
        cs = slice(g * LANES, (g + 1) * LANES)
        for c in range(GMLP_T // C_CHUNK):
            rs = slice(c * C_CHUNK, (c + 1) * C_CHUNK)
            sv = _dot(w, vn[rs, cs]) + bias_ref[:, cs]
            uu = _gelu_tanh(u_ref[rs, cs])
            o_ref[rs, cs] = (uu * sv * _silu(gate_ref[rs, cs])).astype(o_ref.dtype)


def _gmlp(proj2, ln_g, ln_b, w_s, bias_full):
    m, _ = proj2.shape
    width = C_GROUPS * LANES
    T = GMLP_T
    col = lambda c: (lambda i: (i, c // width))
    return pl.pallas_call(
        _gmlp_kernel,
        grid=(m // T,),
        in_specs=[
            pl.BlockSpec((T, width), col(COL_C_U)),
            pl.BlockSpec((T, width), col(COL_C_V)),
            pl.BlockSpec((T, width), col(COL_C_GATE)),
            pl.BlockSpec((1, width), lambda i: (0, 0)),
            pl.BlockSpec((1, width), lambda i: (0, 0)),
            pl.BlockSpec((C_GROUPS, C_CHUNK, C_CHUNK), lambda i: (0, 0, 0)),
            pl.BlockSpec((C_CHUNK, width), lambda i: (0, 0)),
        ],
        out_specs=pl.BlockSpec((T, width), lambda i: (i, 0)),
        out_shape=jax.ShapeDtypeStruct((m, width), MXU_DTYPE),
        compiler_params=_cparams(("arbitrary",)),
        name="gmlp",
    )(proj2, proj2, proj2, ln_g.reshape(1, width), ln_b.reshape(1, width), w_s, bias_full)


def _merge_kernel(a_ref, b_ref, c_ref, wa_ref, wb_ref, wc_ref, ga_ref, gb_ref, gc_ref, o_ref):
    y = _sigmoid(ga_ref[...]) * jnp.dot(a_ref[...], wa_ref[...], preferred_element_type=F32)
    y += _sigmoid(gb_ref[...]) * jnp.dot(b_ref[...], wb_ref[...], preferred_element_type=F32)
    y += _sigmoid(gc_ref[...]) * jnp.dot(c_ref[...], wc_ref[...], preferred_element_type=F32)
    o_ref[...] = y.astype(o_ref.dtype)


def _merge(ya, yb, yc, wa, wb, wc, proj2):
    m = ya.shape[0]
    n = wa.shape[1]
    tm, tn = 512, 512
    gcol = lambda br: (lambda i, j: (i, (COL_MERGE + br * n) // tn + j))
    act = lambda k: pl.BlockSpec((tm, k), lambda i, j: (i, 0))
    wsp = lambda k: pl.BlockSpec((k, tn), lambda i, j: (0, j))
    return pl.pallas_call(
        _merge_kernel,
        grid=(m // tm, n // tn),
        in_specs=[
            act(ya.shape[1]), act(yb.shape[1]), act(yc.shape[1]),
            wsp(wa.shape[0]), wsp(wb.shape[0]), wsp(wc.shape[0]),
            pl.BlockSpec((tm, tn), gcol(0)),
            pl.BlockSpec((tm, tn), gcol(1)),
            pl.BlockSpec((tm, tn), gcol(2)),
        ],
        out_specs=pl.BlockSpec((tm, tn), lambda i, j: (i, j)),
        out_shape=jax.ShapeDtypeStruct((m, n), MXU_DTYPE),
        compiler_params=_cparams(("arbitrary", "arbitrary")),
        name="merge",
    )(ya, yb, yc, wa, wb, wc, proj2, proj2, proj2)


def _post_kernel(x_ref, o_ref, g_ref, gate_ref, y_ref):
    o = o_ref[0]
    ms = jnp.mean(o * o, axis=-1, keepdims=True)
    y_ref[0] = x_ref[0] + gate_ref[0] * (o * lax.rsqrt(ms + NORM_EPS) * g_ref[...])


def _post(x, out3, g_post, mod3):
    bsz, seq, d = x.shape
    ts = 512
    spec = pl.BlockSpec((1, ts, d), lambda b, t: (b, t, 0))
    return pl.pallas_call(
        _post_kernel,
        grid=(bsz, seq // ts),
        in_specs=[
            spec, spec,
            pl.BlockSpec((1, d), lambda b, t: (0, 0)),
            pl.BlockSpec((1, 1, d), lambda b, t: (b, 0, 2)),
        ],
        out_specs=spec,
        out_shape=jax.ShapeDtypeStruct((bsz, seq, d), F32),
        compiler_params=_cparams(("arbitrary", "arbitrary")),
        name="post",
    )(x, out3, g_post.reshape(1, d), mod3)


def _pack_w_in(w_in_l):
    n_qkv = 3 * A_HEADS * A_DK
    w_main = jnp.concatenate([w_in_l[:, :n_qkv], w_in_l[:, n_qkv + 2 * A_HEADS:]], axis=1)
    w_ba = jnp.pad(w_in_l[:, n_qkv:n_qkv + 2 * A_HEADS], ((0, 0), (0, LANES - 2 * A_HEADS)))
    return w_main.astype(MXU_DTYPE), w_ba.astype(MXU_DTYPE)


def _layer(x, mod_l, g_pre, g_post, w_in_l, conv_a, a_log, dt_bias, g_onorm_a,
           ln_c_g, ln_c_b, w_spatial, b_spatial, w_br_a, w_br_b, w_br_c, w_out):
    bsz, seq, d = x.shape
    m = bsz * seq
    mod3 = mod_l[:bsz].reshape(bsz, 1, 3 * d)
    w_main, w_ba = _pack_w_in(w_in_l)

    h, ba = _prenorm(x, g_pre, mod3, w_ba)
    proj = _matmul(h.reshape(m, d), w_main, F32, 1024, 1024, "in_proj")
    proj3 = proj.reshape(bsz, seq, MAIN_WIDTH)

    cw = conv_a.reshape(A_CONV, 3, A_HEADS, A_DK).transpose(2, 1, 0, 3)
    cw = jnp.pad(cw, ((0, 0), (0, 0), (0, 8 - A_CONV), (0, 0)))
    hp = jnp.stack([a_log, dt_bias], axis=1)
    hp = jnp.broadcast_to(jnp.pad(hp, ((0, 0), (0, 6)))[:, :, None], (A_HEADS, 8, LANES))
    ya = _gdn(proj3, ba, cw, hp, g_onorm_a.reshape(1, A_DK))

    parts = []
    for gi, (_, dilation) in enumerate(B_GROUPS):
        parts.extend(_attn_group(proj3, gi, dilation))
    yb = _attn_combine(parts, proj3)

    bias_full = jnp.repeat(b_spatial.T, LANES, axis=1)
    yc = _gmlp(proj, ln_c_g, ln_c_b, w_spatial, bias_full)

    y = _merge(ya.reshape(m, -1), yb.reshape(m, -1), yc,
               w_br_a.astype(MXU_DTYPE), w_br_b.astype(MXU_DTYPE), w_br_c.astype(MXU_DTYPE), proj)
    out = _matmul(y, w_out.astype(MXU_DTYPE), F32, 1024, 1024, "out_proj")
    return _post(x, out.reshape(bsz, seq, d), g_post, mod3)


def kernel(x, c, w_ada, b_ada, g_pre, g_post, w_in, conv_a, a_log, dt_bias, g_onorm_a,
           ln_c_g, ln_c_b, w_spatial, b_spatial, w_br_a, w_br_b, w_br_c, w_out):
    bsz = x.shape[0]
    depth = w_ada.shape[0]
    c_pad = jnp.pad(c, ((0, 8 - bsz), (0, 0)))
    mod = _ada(c_pad, w_ada, b_ada)
    for l in range(depth):
        x = _layer(x, mod[l], g_pre[l], g_post[l], w_in[l], conv_a[l], a_log[l], dt_bias[l],
                   g_onorm_a[l], ln_c_g[l], ln_c_b[l], w_spatial[l], b_spatial[l],
                   w_br_a[l], w_br_b[l], w_br_c[l], w_out[l])
    return x
```

```python
import functools

import jax
import jax.numpy as jnp
from jax import lax
from jax.experimental import pallas as pl
from jax.experimental.pallas import tpu as pltpu

F32 = jnp.float32
MXU_DTYPE = jnp.bfloat16
NORM_EPS = 1e-6

D_MODEL = 4096
A_HEADS = 16
A_DK = 128
A_CONV = 4
A_CHUNK = 64
B_GROUPS = ((128, 1), (512, 4), (2048, 16))
B_HPG = 8
B_HEADS = 24
B_HD = 128
B_BLOCK = 128
ALIBI_MAX_BIAS = 8.0
C_CHUNK = 128
C_GROUPS = 16
N_BRANCH = 3

COL_A_QKV, COL_A_GATE, COL_B_QKV, COL_B_GATE = 0, 6144, 8192, 11264
COL_C_U, COL_C_V, COL_C_GATE, COL_MERGE = 12288, 14336, 16384, 18432
MAIN_WIDTH = 30720
LANES = 128

VMEM_LIMIT = 56 * 1024 * 1024


def _cparams(sem):
    return pltpu.CompilerParams(dimension_semantics=sem, vmem_limit_bytes=VMEM_LIMIT)


def _mx(x):
    return x.astype(MXU_DTYPE)


def _dot(a, b):
    return jnp.dot(_mx(a), _mx(b), preferred_element_type=F32)


def _dot_nt(a, b):
    return lax.dot_general(_mx(a), _mx(b), (((1,), (1,)), ((), ())), preferred_element_type=F32)


def _dot_tn(a, b):
    return lax.dot_general(_mx(a), _mx(b), (((0,), (0,)), ((), ())), preferred_element_type=F32)


def _dot_f32(a, b):
    return jnp.dot(a, b, preferred_element_type=F32, precision=lax.Precision.HIGHEST)


def _sigmoid(x):
    return 1.0 / (1.0 + jnp.exp(-x))


def _silu(x):
    return x * _sigmoid(x)


def _gelu_tanh(x):
    return 0.5 * x * (1.0 + jnp.tanh(0.7978845608028654 * (x + 0.044715 * (x * x * x))))


def _ada_kernel(c_ref, w_ref, b_ref, o_ref):
    cond = _silu(c_ref[...])
    o_ref[0] = _dot(cond, w_ref[0]) + b_ref[0]


def _ada(c_pad, w_ada, b_ada):
    depth, d, n = w_ada.shape
    tn = 512
    return pl.pallas_call(
        _ada_kernel,
        grid=(depth, n // tn),
        in_specs=[
            pl.BlockSpec((8, d), lambda l, j: (0, 0)),
            pl.BlockSpec((1, d, tn), lambda l, j: (l, 0, j)),
            pl.BlockSpec((1, 1, tn), lambda l, j: (l, 0, j)),
        ],
        out_specs=pl.BlockSpec((1, 8, tn), lambda l, j: (l, 0, j)),
        out_shape=jax.ShapeDtypeStruct((depth, 8, n), F32),
        compiler_params=_cparams(("arbitrary", "arbitrary")),
        name="ada_mod",
    )(c_pad, w_ada, b_ada.reshape(depth, 1, n))


def _prenorm_kernel(x_ref, g_ref, shift_ref, scale_ref, wba_ref, h_ref, ba_ref):
    x = x_ref[0]
    ms = jnp.mean(x * x, axis=-1, keepdims=True)
    y = x * lax.rsqrt(ms + NORM_EPS) * g_ref[...]
    h = y * (1.0 + scale_ref[0]) + shift_ref[0]
    hb = _mx(h)
    h_ref[0] = hb
    ba_ref[0] = jnp.dot(hb, wba_ref[...], preferred_element_type=F32)


def _prenorm(x, g_pre, mod3, w_ba):
    bsz, seq, d = x.shape
    ts = 512
    return pl.pallas_call(
        _prenorm_kernel,
        grid=(bsz, seq // ts),
        in_specs=[
            pl.BlockSpec((1, ts, d), lambda b, t: (b, t, 0)),
            pl.BlockSpec((1, d), lambda b, t: (0, 0)),
            pl.BlockSpec((1, 1, d), lambda b, t: (b, 0, 0)),
            pl.BlockSpec((1, 1, d), lambda b, t: (b, 0, 1)),
            pl.BlockSpec((d, LANES), lambda b, t: (0, 0)),
        ],
        out_specs=[
            pl.BlockSpec((1, ts, d), lambda b, t: (b, t, 0)),
            pl.BlockSpec((1, ts, LANES), lambda b, t: (b, t, 0)),
        ],
        out_shape=[
            jax.ShapeDtypeStruct((bsz, seq, d), MXU_DTYPE),
            jax.ShapeDtypeStruct((bsz, seq, LANES), F32),
        ],
        compiler_params=_cparams(("arbitrary", "arbitrary")),
        name="prenorm",
    )(x, g_pre.reshape(1, d), mod3, mod3, w_ba)


def _matmul_kernel(a_ref, b_ref, o_ref):
    o_ref[...] = jnp.dot(a_ref[...], b_ref[...], preferred_element_type=F32).astype(o_ref.dtype)


def _matmul(a, b, out_dtype, tm, tn, name):
    m, k = a.shape
    _, n = b.shape
    return pl.pallas_call(
        _matmul_kernel,
        grid=(m // tm, n // tn),
        in_specs=[
            pl.BlockSpec((tm, k), lambda i, j: (i, 0)),
            pl.BlockSpec((k, tn), lambda i, j: (0, j)),
        ],
        out_specs=pl.BlockSpec((tm, tn), lambda i, j: (i, j)),
        out_shape=jax.ShapeDtypeStruct((m, n), out_dtype),
        compiler_params=_cparams(("arbitrary", "arbitrary")),
        name=name,
    )(a, b)


GDN_T = 256
GDN_NC = GDN_T // A_CHUNK
GDN_HB = 2


def _block_masks(n):
    i = lax.broadcasted_iota(jnp.int32, (n, n), 0)
    j = lax.broadcasted_iota(jnp.int32, (n, n), 1)
    lower = i >= j
    strict = i > j
    same = lambda s: (i >> (s.bit_length() - 1)) == (j >> (s.bit_length() - 1))
    return i, j, lower, strict, same


def _gdn_kernel(q_ref, k_ref, v_ref, gate_ref, ba_ref, cw_ref, hp_ref, gon_ref, o_ref,
                state_ref, hist_ref):
    t = pl.program_id(2)

    @pl.when(t == 0)
    def _():
        state_ref[...] = jnp.zeros_like(state_ref)
        hist_ref[:, 0:8, :] = jnp.zeros((3 * GDN_HB, 8, LANES), F32)

    for hh in range(GDN_HB):
        _gdn_head(hh, q_ref, k_ref, v_ref, gate_ref, ba_ref, cw_ref, hp_ref, gon_ref, o_ref,
                  state_ref, hist_ref)


def _gdn_head(hh, q_ref, k_ref, v_ref, gate_ref, ba_ref, cw_ref, hp_ref, gon_ref, o_ref,
              state_ref, hist_ref):
    h = pl.program_id(1) * GDN_HB + hh
    T = GDN_T
    ls = slice(hh * LANES, (hh + 1) * LANES)

    def conv(p, x_ref):
        hp_ = hh * 3 + p
        hist_ref[hp_, 8:, :] = x_ref[0, :, ls]
        w = cw_ref[hh, p]
        acc = hist_ref[hp_, pl.ds(8, T), :] * w[3:4, :]
        acc += hist_ref[hp_, pl.ds(7, T), :] * w[2:3, :]
        acc += hist_ref[hp_, pl.ds(6, T), :] * w[1:2, :]
        acc += hist_ref[hp_, pl.ds(5, T), :] * w[0:1, :]
        hist_ref[hp_, 0:8, :] = hist_ref[hp_, T:T + 8, :]
        return _silu(acc)

    q = conv(0, q_ref)
    k = conv(1, k_ref)
    v = conv(2, v_ref)
    q = q * lax.rsqrt(jnp.sum(q * q, axis=-1, keepdims=True) + NORM_EPS) * (A_DK ** -0.5)
    k = k * lax.rsqrt(jnp.sum(k * k, axis=-1, keepdims=True) + NORM_EPS)

    ba = ba_ref[0]
    lane = lax.broadcasted_iota(jnp.int32, (T, LANES), 1)
    bl = jnp.sum(jnp.where(lane == h, ba, 0.0), axis=-1, keepdims=True)
    al = jnp.sum(jnp.where(lane == h + A_HEADS, ba, 0.0), axis=-1, keepdims=True)
    beta = jnp.broadcast_to(_sigmoid(bl), (T, LANES))
    a_log = hp_ref[hh, 0:1, :]
    dt_b = hp_ref[hh, 1:2, :]
    z = jnp.broadcast_to(al, (T, LANES)) + dt_b
    softplus = jnp.maximum(z, 0.0) + jnp.log(1.0 + jnp.exp(-jnp.abs(z)))
    g = -jnp.exp(a_log) * softplus

    _, _, lower, strict, same = _block_masks(T)
    in_chunk = same(A_CHUNK)
    tri = jnp.where(lower & in_chunk, 1.0, 0.0)
    ones_bd = jnp.where(in_chunk, 1.0, 0.0)
    gc = _dot_f32(tri, g)
    glast = _dot_f32(ones_bd, g)
    gc2 = jnp.concatenate([gc, gc], axis=1)
    diff = gc2 - gc2.T
    lo_mask = lower & in_chunk
    decay = jnp.where(lo_mask, jnp.exp(jnp.where(lo_mask, diff, 0.0)), 0.0)

    kb = k * beta
    a_mat = jnp.where(strict & in_chunk, _dot_nt(kb, k) * decay, 0.0)
    qk = jnp.where(lo_mask, _dot_nt(q, k) * decay, 0.0)
    eg = jnp.exp(gc)
    rhs = jnp.concatenate([v * beta, kb * eg], axis=1)

    s16, s32 = same(16), same(32)
    eye = jnp.where(lower & ~strict, 1.0, 0.0)
    d = jnp.where(s16, a_mat, 0.0)
    d2 = _dot(d, d)
    d4 = _dot(d2, d2)
    d8 = _dot(d4, d4)
    p = eye - d
    p = p + _dot(p, d2)
    p = p + _dot(p, d4)
    p = p + _dot(p, d8)
    e1 = jnp.where(s32 & ~s16, a_mat, 0.0)
    p = p - _dot(_dot(p, e1), p)
    e2 = jnp.where(in_chunk & ~s32, a_mat, 0.0)
    p = p - _dot(_dot(p, e2), p)
    wy = _dot(p, rhs)
    u = wy[:, :LANES]
    k_cum = wy[:, LANES:]

    q_dec = q * eg
    k_dec = k * jnp.exp(glast - gc)
    cd = jnp.exp(glast)

    gon = gon_ref[...]
    gate = gate_ref[0, :, ls]
    state = state_ref[hh]
    for c in range(GDN_NC):
        r = slice(c * A_CHUNK, (c + 1) * A_CHUNK)
        kq = jnp.concatenate([k_cum[r], q_dec[r]], axis=0)
        res = _dot(kq, state)
        v_new = u[r] - res[:A_CHUNK]
        o_c = res[A_CHUNK:] + _dot(qk[r, c * A_CHUNK:(c + 1) * A_CHUNK], v_new)
        state = state * cd[c * A_CHUNK:c * A_CHUNK + 1, :] + _dot_tn(k_dec[r], v_new)
        ms = jnp.mean(o_c * o_c, axis=-1, keepdims=True)
        o_n = o_c * lax.rsqrt(ms + NORM_EPS) * gon
        o_ref[0, r, ls] = (o_n * _silu(gate[r])).astype(o_ref.dtype)
    state_ref[hh] = state


def _gdn(proj3, ba3, cw, hp, gon):
    bsz, seq, _ = proj3.shape
    T = GDN_T
    hb = GDN_HB
    wblk = hb * LANES
    qb, kb_, vb, gb = (COL_A_QKV // wblk, COL_A_QKV // wblk + A_HEADS // hb,
                       COL_A_QKV // wblk + 2 * A_HEADS // hb, COL_A_GATE // wblk)
    return pl.pallas_call(
        _gdn_kernel,
        grid=(bsz, A_HEADS // hb, seq // T),
        in_specs=[
            pl.BlockSpec((1, T, wblk), lambda b, h, t: (b, t, qb + h)),
            pl.BlockSpec((1, T, wblk), lambda b, h, t: (b, t, kb_ + h)),
            pl.BlockSpec((1, T, wblk), lambda b, h, t: (b, t, vb + h)),
            pl.BlockSpec((1, T, wblk), lambda b, h, t: (b, t, gb + h)),
            pl.BlockSpec((1, T, LANES), lambda b, h, t: (b, t, 0)),
            pl.BlockSpec((hb, 3, 8, LANES), lambda b, h, t: (h, 0, 0, 0)),
            pl.BlockSpec((hb, 8, LANES), lambda b, h, t: (h, 0, 0)),
            pl.BlockSpec((1, LANES), lambda b, h, t: (0, 0)),
        ],
        out_specs=pl.BlockSpec((1, T, wblk), lambda b, h, t: (b, t, h)),
        out_shape=jax.ShapeDtypeStruct((bsz, seq, A_HEADS * A_DK), MXU_DTYPE),
        scratch_shapes=[
            pltpu.VMEM((hb, A_DK, A_DK), F32),
            pltpu.VMEM((3 * hb, T + 8, LANES), F32),
        ],
        compiler_params=_cparams(("arbitrary", "arbitrary", "arbitrary")),
        name="gdn",
    )(proj3, proj3, proj3, proj3, ba3, cw, hp, gon)


def _attn_kernel(q_ref, kp_ref, kc_ref, vp_ref, vc_ref, o_ref, lse_ref, *, dilation, group):
    n = pl.program_id(2)
    blk = B_BLOCK
    i = lax.broadcasted_iota(jnp.int32, (blk, 2 * blk), 0)
    j = lax.broadcasted_iota(jnp.int32, (blk, 2 * blk), 1)
    rel = i + blk - j
    valid = (rel >= 0) & (rel <= blk) & jnp.logical_or(j >= blk, n > 0)
    relf = rel.astype(F32)
    neg = jnp.where(valid, 0.0, -jnp.inf)
    scale = B_HD ** -0.5
    for hh in range(B_HPG):
        head = group * B_HPG + hh
        slope = 2.0 ** (-ALIBI_MAX_BIAS * (head + 1) / B_HEADS)
        sl = slice(hh * B_HD, (hh + 1) * B_HD)
        kk = jnp.concatenate([kp_ref[0, :, sl], kc_ref[0, :, sl]], axis=0)
        vv = jnp.concatenate([vp_ref[0, :, sl], vc_ref[0, :, sl]], axis=0)
        s = _dot_nt(q_ref[0, :, sl], kk) * scale + relf * (-slope * dilation) + neg
        m = jnp.max(s, axis=-1, keepdims=True)
        p = jnp.exp(s - m)
        l = jnp.sum(p, axis=-1, keepdims=True)
        acc = _dot(p, vv)
        o_ref[0, :, sl] = acc / l
        lse_ref[0, :, sl] = jnp.broadcast_to(m + jnp.log(l), (blk, B_HD))


def _attn_group(arr3, group, dilation, qcol):
    bsz, seq, width = arr3.shape
    rows = seq // dilation
    nb = rows // B_BLOCK
    wblk = B_HPG * B_HD
    per_row = width // wblk
    view = arr3.reshape(bsz, rows, dilation * width)
    qc = qcol // wblk
    kc = qc + 1
    vc = kc + 1
    cur = lambda c: (lambda b, r, n: (b, n, r * per_row + c))
    prev = lambda c: (lambda b, r, n: (b, jnp.maximum(n - 1, 0), r * per_row + c))
    blk = (1, B_BLOCK, wblk)
    o, lse = pl.pallas_call(
        functools.partial(_attn_kernel, dilation=dilation, group=group),
        grid=(bsz, dilation, nb),
        in_specs=[
            pl.BlockSpec(blk, cur(qc)),
            pl.BlockSpec(blk, prev(kc)),
            pl.BlockSpec(blk, cur(kc)),
            pl.BlockSpec(blk, prev(vc)),
            pl.BlockSpec(blk, cur(vc)),
        ],
        out_specs=[
            pl.BlockSpec(blk, lambda b, r, n: (b, n, r)),
            pl.BlockSpec(blk, lambda b, r, n: (b, n, r)),
        ],
        out_shape=[
            jax.ShapeDtypeStruct((bsz, rows, dilation * wblk), F32),
            jax.ShapeDtypeStruct((bsz, rows, dilation * wblk), F32),
        ],
        compiler_params=_cparams(("arbitrary", "arbitrary", "arbitrary")),
        name=f"attn_g{group}",
    )(view, view, view, view, view)
    return o.reshape(bsz, seq, wblk), lse.reshape(bsz, seq, wblk)


def _attn_combine_kernel(o0, l0, o1, l1, o2, l2, gate_ref, out_ref):
    a0, a1, a2 = l0[0], l1[0], l2[0]
    m = jnp.maximum(jnp.maximum(a0, a1), a2)
    w0, w1, w2 = jnp.exp(a0 - m), jnp.exp(a1 - m), jnp.exp(a2 - m)
    o = (w0 * o0[0] + w1 * o1[0] + w2 * o2[0]) / (w0 + w1 + w2)
    out_ref[0] = (o * _silu(gate_ref[0])).astype(out_ref.dtype)


def _attn_combine(parts, proj3):
    bsz, seq, _ = proj3.shape
    wblk = B_HPG * B_HD
    ts = 512
    spec = pl.BlockSpec((1, ts, wblk), lambda b, t: (b, t, 0))
    return pl.pallas_call(
        _attn_combine_kernel,
        grid=(bsz, seq // ts),
        in_specs=[spec] * 6 + [pl.BlockSpec((1, ts, wblk), lambda b, t: (b, t, COL_B_GATE // wblk))],
        out_specs=spec,
        out_shape=jax.ShapeDtypeStruct((bsz, seq, wblk), MXU_DTYPE),
        compiler_params=_cparams(("arbitrary", "arbitrary")),
        name="attn_combine",
    )(*parts, proj3)


GMLP_T = 256


def _gmlp_kernel(u_ref, v_ref, gate_ref, lng_ref, lnb_ref, ws_ref, bias_ref, o_ref):
    width = C_GROUPS * LANES
    vv = _gelu_tanh(v_ref[...])
    mu = jnp.mean(vv, axis=-1, keepdims=True)
    xc = vv - mu
    var = jnp.mean(xc * xc, axis=-1, keepdims=True)
    vn = xc * lax.rsqrt(var + NORM_EPS) * lng_ref[...] + lnb_ref[...]
    ti = lax.broadcasted_iota(jnp.int32, (C_CHUNK, C_CHUNK), 0)
    si = lax.broadcasted_iota(jnp.int32, (C_CHUNK, C_CHUNK), 1)
    causal = ti >= si
    for g in range(C_GROUPS):
        w = jnp.where(causal, ws_ref[g], 0.0)
        cs = slice(g * LANES, (g + 1) * LANES)
        for c in range(GMLP_T // C_CHUNK):
            rs = slice(c * C_CHUNK, (c + 1) * C_CHUNK)
            sv = _dot(w, vn[rs, cs]) + bias_ref[:, cs]
            uu = _gelu_tanh(u_ref[rs, cs])
            o_ref[rs, cs] = (uu * sv * _silu(gate_ref[rs, cs])).astype(o_ref.dtype)


def _gmlp(proj2, ln_g, ln_b, w_s, bias_full):
    m, _ = proj2.shape
    width = C_GROUPS * LANES
    T = GMLP_T
    col = lambda c: (lambda i: (i, c // width))
    return pl.pallas_call(
        _gmlp_kernel,
        grid=(m // T,),
        in_specs=[
            pl.BlockSpec((T, width), col(COL_C_U)),
            pl.BlockSpec((T, width), col(COL_C_V)),
            pl.BlockSpec((T, width), col(COL_C_GATE)),
            pl.BlockSpec((1, width), lambda i: (0, 0)),
            pl.BlockSpec((1, width), lambda i: (0, 0)),
            pl.BlockSpec((C_GROUPS, C_CHUNK, C_CHUNK), lambda i: (0, 0, 0)),
            pl.BlockSpec((C_CHUNK, width), lambda i: (0, 0)),
        ],
        out_specs=pl.BlockSpec((T, width), lambda i: (i, 0)),
        out_shape=jax.ShapeDtypeStruct((m, width), MXU_DTYPE),
        compiler_params=_cparams(("arbitrary",)),
        name="gmlp",
    )(proj2, proj2, proj2, ln_g.reshape(1, width), ln_b.reshape(1, width), w_s, bias_full)


def _merge_kernel(a_ref, b_ref, c_ref, wa_ref, wb_ref, wc_ref, ga_ref, gb_ref, gc_ref, o_ref):
    y = _sigmoid(ga_ref[...]) * jnp.dot(a_ref[...], wa_ref[...], preferred_element_type=F32)
    y += _sigmoid(gb_ref[...]) * jnp.dot(b_ref[...], wb_ref[...], preferred_element_type=F32)
    y += _sigmoid(gc_ref[...]) * jnp.dot(c_ref[...], wc_ref[...], preferred_element_type=F32)
    o_ref[...] = y.astype(o_ref.dtype)


def _merge(ya, yb, yc, wa, wb, wc, proj2):
    m = ya.shape[0]
    n = wa.shape[1]
    tm, tn = 512, 512
    gcol = lambda br: (lambda i, j: (i, (COL_MERGE + br * n) // tn + j))
    act = lambda k: pl.BlockSpec((tm, k), lambda i, j: (i, 0))
    wsp = lambda k: pl.BlockSpec((k, tn), lambda i, j: (0, j))
    return pl.pallas_call(
        _merge_kernel,
        grid=(m // tm, n // tn),
        in_specs=[
            act(ya.shape[1]), act(yb.shape[1]), act(yc.shape[1]),
            wsp(wa.shape[0]), wsp(wb.shape[0]), wsp(wc.shape[0]),
            pl.BlockSpec((tm, tn), gcol(0)),
            pl.BlockSpec((tm, tn), gcol(1)),
            pl.BlockSpec((tm, tn), gcol(2)),
        ],
        out_specs=pl.BlockSpec((tm, tn), lambda i, j: (i, j)),
        out_shape=jax.ShapeDtypeStruct((m, n), MXU_DTYPE),
        compiler_params=_cparams(("arbitrary", "arbitrary")),
        name="merge",
    )(ya, yb, yc, wa, wb, wc, proj2, proj2, proj2)


def _post_kernel(x_ref, o_ref, g_ref, gate_ref, y_ref):
    o = o_ref[0]
    ms = jnp.mean(o * o, axis=-1, keepdims=True)
    y_ref[0] = x_ref[0] + gate_ref[0] * (o * lax.rsqrt(ms + NORM_EPS) * g_ref[...])


def _post(x, out3, g_post, mod3):
    bsz, seq, d = x.shape
    ts = 512
    spec = pl.BlockSpec((1, ts, d), lambda b, t: (b, t, 0))
    return pl.pallas_call(
        _post_kernel,
        grid=(bsz, seq // ts),
        in_specs=[
            spec, spec,
            pl.BlockSpec((1, d), lambda b, t: (0, 0)),
            pl.BlockSpec((1, 1, d), lambda b, t: (b, 0, 2)),
        ],
        out_specs=spec,
        out_shape=jax.ShapeDtypeStruct((bsz, seq, d), F32),
        compiler_params=_cparams(("arbitrary", "arbitrary")),
        name="post",
    )(x, out3, g_post.reshape(1, d), mod3)


def _pack_w_in(w_in_l):
    n_qkv = 3 * A_HEADS * A_DK
    ba0 = n_qkv
    ag0 = ba0 + 2 * A_HEADS
    bq0 = ag0 + A_HEADS * A_DK
    hw = B_HEADS * B_HD
    gw = B_HPG * B_HD
    rest0 = bq0 + 3 * hw
    qkv = lambda gi: [w_in_l[:, bq0 + p * hw + gi * gw: bq0 + p * hw + (gi + 1) * gw] for p in range(3)]
    w_main = jnp.concatenate([w_in_l[:, :n_qkv], w_in_l[:, ag0:bq0]] + qkv(0) + [w_in_l[:, rest0:]], axis=1)
    w_ba = jnp.pad(w_in_l[:, ba0:ag0], ((0, 0), (0, LANES - 2 * A_HEADS)))
    w_groups = [jnp.concatenate(qkv(gi), axis=1).astype(MXU_DTYPE) for gi in range(1, len(B_GROUPS))]
    return w_main.astype(MXU_DTYPE), w_ba.astype(MXU_DTYPE), w_groups


def _layer(x, mod_l, g_pre, g_post, w_in_l, conv_a, a_log, dt_bias, g_onorm_a,
           ln_c_g, ln_c_b, w_spatial, b_spatial, w_br_a, w_br_b, w_br_c, w_out):
    bsz, seq, d = x.shape
    m = bsz * seq
    mod3 = mod_l[:bsz].reshape(bsz, 1, 3 * d)
    w_main, w_ba, w_groups = _pack_w_in(w_in_l)

    h, ba = _prenorm(x, g_pre, mod3, w_ba)
    h2 = h.reshape(m, d)
    proj = _matmul(h2, w_main, F32, 1024, 1024, "in_proj")
    proj3 = proj.reshape(bsz, seq, MAIN_WIDTH)
    proj_groups = [_matmul(h2, w, F32, 1024, 1024, f"in_proj_g{gi + 1}").reshape(bsz, seq, -1)
                   for gi, w in enumerate(w_groups)]

    cw = conv_a.reshape(A_CONV, 3, A_HEADS, A_DK).transpose(2, 1, 0, 3)
    cw = jnp.pad(cw, ((0, 0), (0, 0), (0, 8 - A_CONV), (0, 0)))
    hp = jnp.stack([a_log, dt_bias], axis=1)
    hp = jnp.broadcast_to(jnp.pad(hp, ((0, 0), (0, 6)))[:, :, None], (A_HEADS, 8, LANES))
    ya = _gdn(proj3, ba, cw, hp, g_onorm_a.reshape(1, A_DK))

    parts = []
    for gi, (_, dilation) in enumerate(B_GROUPS):
        if gi == 0:
            parts.extend(_attn_group(proj3, gi, dilation, COL_B_QKV))
        else:
            parts.extend(_attn_group(proj_groups[gi - 1], gi, dilation, 0))
    yb = _attn_combine(parts, proj3)

    bias_full = jnp.repeat(b_spatial.T, LANES, axis=1)
    yc = _gmlp(proj, ln_c_g, ln_c_b, w_spatial, bias_full)

    y = _merge(ya.reshape(m, -1), yb.reshape(m, -1), yc,
               w_br_a.astype(MXU_DTYPE), w_br_b.astype(MXU_DTYPE), w_br_c.astype(MXU_DTYPE), proj)
    out = _matmul(y, w_out.astype(MXU_DTYPE), F32, 1024, 1024, "out_proj")
    return _post(x, out.reshape(bsz, seq, d), g_post, mod3)


def kernel(x, c, w_ada, b_ada, g_pre, g_post, w_in, conv_a, a_log, dt_bias, g_onorm_a,
           ln_c_g, ln_c_b, w_spatial, b_spatial, w_br_a, w_br_b, w_br_c, w_out):
    bsz = x.shape[0]
    depth = w_ada.shape[0]
    c_pad = jnp.pad(c, ((0, 8 - bsz), (0, 0)))
    mod = _ada(c_pad, w_ada, b_ada)
    for l in range(depth):
        x = _layer(x, mod[l], g_pre[l], g_post[l], w_in[l], conv_a[l], a_log[l], dt_bias[l],
                   g_onorm_a[l], ln_c_g[l], ln_c_b[l], w_spatial[l], b_spatial[l],
                   w_br_a[l], w_br_b[l], w_br_c[l], w_out[l])
    return x
```

```python
import functools

import jax
import jax.numpy as jnp
from jax import lax
from jax.experimental import pallas as pl
from jax.experimental.pallas import tpu as pltpu

F32 = jnp.float32
MXU_DTYPE = jnp.bfloat16
NORM_EPS = 1e-6

D_MODEL = 4096
A_HEADS = 16
A_DK = 128
A_CONV = 4
A_CHUNK = 64
B_GROUPS = ((128, 1), (512, 4), (2048, 16))
B_HPG = 8
B_HEADS = 24
B_HD = 128
B_BLOCK = 128
ALIBI_MAX_BIAS = 8.0
C_CHUNK = 128
C_GROUPS = 16
N_BRANCH = 3

COL_A_QKV, COL_A_GATE, COL_B_QKV, COL_B_GATE = 0, 6144, 8192, 11264
COL_C_U, COL_C_V, COL_C_GATE, COL_MERGE = 12288, 14336, 16384, 18432
MAIN_WIDTH = 30720
LANES = 128

VMEM_LIMIT = 56 * 1024 * 1024


def _cparams(sem):
    return pltpu.CompilerParams(dimension_semantics=sem, vmem_limit_bytes=VMEM_LIMIT)


def _mx(x):
    return x.astype(MXU_DTYPE)


def _dot(a, b):
    return jnp.dot(_mx(a), _mx(b), preferred_element_type=F32)


def _dot_nt(a, b):
    return lax.dot_general(_mx(a), _mx(b), (((1,), (1,)), ((), ())), preferred_element_type=F32)


def _dot_tn(a, b):
    return lax.dot_general(_mx(a), _mx(b), (((0,), (0,)), ((), ())), preferred_element_type=F32)


def _dot_f32(a, b):
    return jnp.dot(a, b, preferred_element_type=F32, precision=lax.Precision.HIGHEST)


def _sigmoid(x):
    return 1.0 / (1.0 + jnp.exp(-x))


def _silu(x):
    return x * _sigmoid(x)


def _gelu_tanh(x):
    return 0.5 * x * (1.0 + jnp.tanh(0.7978845608028654 * (x + 0.044715 * (x * x * x))))


def _ada_kernel(c_ref, w_ref, b_ref, o_ref):
    cond = _silu(c_ref[...])
    o_ref[0] = _dot(cond, w_ref[0]) + b_ref[0]


def _ada(c_pad, w_ada, b_ada):
    depth, d, n = w_ada.shape
    tn = 512
    return pl.pallas_call(
        _ada_kernel,
        grid=(depth, n // tn),
        in_specs=[
            pl.BlockSpec((8, d), lambda l, j: (0, 0)),
            pl.BlockSpec((1, d, tn), lambda l, j: (l, 0, j)),
            pl.BlockSpec((1, 1, tn), lambda l, j: (l, 0, j)),
        ],
        out_specs=pl.BlockSpec((1, 8, tn), lambda l, j: (l, 0, j)),
        out_shape=jax.ShapeDtypeStruct((depth, 8, n), F32),
        compiler_params=_cparams(("arbitrary", "arbitrary")),
        name="ada_mod",
    )(c_pad, w_ada, b_ada.reshape(depth, 1, n))


def _prenorm_kernel(x_ref, g_ref, shift_ref, scale_ref, wba_ref, h_ref, ba_ref):
    x = x_ref[0]
    ms = jnp.mean(x * x, axis=-1, keepdims=True)
    y = x * lax.rsqrt(ms + NORM_EPS) * g_ref[...]
    h = y * (1.0 + scale_ref[0]) + shift_ref[0]
    hb = _mx(h)
    h_ref[0] = hb
    ba_ref[0] = jnp.dot(hb, wba_ref[...], preferred_element_type=F32)


def _prenorm(x, g_pre, mod3, w_ba):
    bsz, seq, d = x.shape
    ts = 512
    return pl.pallas_call(
        _prenorm_kernel,
        grid=(bsz, seq // ts),
        in_specs=[
            pl.BlockSpec((1, ts, d), lambda b, t: (b, t, 0)),
            pl.BlockSpec((1, d), lambda b, t: (0, 0)),
            pl.BlockSpec((1, 1, d), lambda b, t: (b, 0, 0)),
            pl.BlockSpec((1, 1, d), lambda b, t: (b, 0, 1)),
            pl.BlockSpec((d, LANES), lambda b, t: (0, 0)),
        ],
        out_specs=[
            pl.BlockSpec((1, ts, d), lambda b, t: (b, t, 0)),
            pl.BlockSpec((1, ts, LANES), lambda b, t: (b, t, 0)),
        ],
        out_shape=[
            jax.ShapeDtypeStruct((bsz, seq, d), MXU_DTYPE),
            jax.ShapeDtypeStruct((bsz, seq, LANES), F32),
        ],
        compiler_params=_cparams(("arbitrary", "arbitrary")),
        name="prenorm",
    )(x, g_pre.reshape(1, d), mod3, mod3, w_ba)


def _matmul_kernel(a_ref, b_ref, o_ref):
    o_ref[...] = jnp.dot(a_ref[...], b_ref[...], preferred_element_type=F32).astype(o_ref.dtype)


def _matmul(a, b, out_dtype, tm, tn, name):
    m, k = a.shape
    _, n = b.shape
    return pl.pallas_call(
        _matmul_kernel,
        grid=(m // tm, n // tn),
        in_specs=[
            pl.BlockSpec((tm, k), lambda i, j: (i, 0)),
            pl.BlockSpec((k, tn), lambda i, j: (0, j)),
        ],
        out_specs=pl.BlockSpec((tm, tn), lambda i, j: (i, j)),
        out_shape=jax.ShapeDtypeStruct((m, n), out_dtype),
        compiler_params=_cparams(("arbitrary", "arbitrary")),
        name=name,
    )(a, b)


GDN_T = 256
GDN_NC = GDN_T // A_CHUNK
GDN_HB = 2


def _block_masks(n):
    i = lax.broadcasted_iota(jnp.int32, (n, n), 0)
    j = lax.broadcasted_iota(jnp.int32, (n, n), 1)
    lower = i >= j
    strict = i > j
    same = lambda s: (i >> (s.bit_length() - 1)) == (j >> (s.bit_length() - 1))
    return i, j, lower, strict, same


def _gdn_kernel(q_ref, k_ref, v_ref, gate_ref, ba_ref, cw_ref, hp_ref, gon_ref, o_ref,
                state_ref, hist_ref):
    t = pl.program_id(2)

    @pl.when(t == 0)
    def _():
        state_ref[...] = jnp.zeros_like(state_ref)
        hist_ref[:, 0:8, :] = jnp.zeros((3 * GDN_HB, 8, LANES), F32)

    for hh in range(GDN_HB):
        _gdn_head(hh, q_ref, k_ref, v_ref, gate_ref, ba_ref, cw_ref, hp_ref, gon_ref, o_ref,
                  state_ref, hist_ref)


def _gdn_head(hh, q_ref, k_ref, v_ref, gate_ref, ba_ref, cw_ref, hp_ref, gon_ref, o_ref,
              state_ref, hist_ref):
    h = pl.program_id(1) * GDN_HB + hh
    T = GDN_T
    ls = slice(hh * LANES, (hh + 1) * LANES)

    def conv(p, x_ref):
        hp_ = hh * 3 + p
        hist_ref[hp_, 8:, :] = x_ref[0, :, ls].astype(F32)
        w = cw_ref[hh, p]
        acc = hist_ref[hp_, pl.ds(8, T), :] * w[3:4, :]
        acc += hist_ref[hp_, pl.ds(7, T), :] * w[2:3, :]
        acc += hist_ref[hp_, pl.ds(6, T), :] * w[1:2, :]
        acc += hist_ref[hp_, pl.ds(5, T), :] * w[0:1, :]
        hist_ref[hp_, 0:8, :] = hist_ref[hp_, T:T + 8, :]
        return _silu(acc)

    q = conv(0, q_ref)
    k = conv(1, k_ref)
    v = conv(2, v_ref)
    q = q * lax.rsqrt(jnp.sum(q * q, axis=-1, keepdims=True) + NORM_EPS) * (A_DK ** -0.5)
    k = k * lax.rsqrt(jnp.sum(k * k, axis=-1, keepdims=True) + NORM_EPS)

    ba = ba_ref[0]
    lane = lax.broadcasted_iota(jnp.int32, (T, LANES), 1)
    bl = jnp.sum(jnp.where(lane == h, ba, 0.0), axis=-1, keepdims=True)
    al = jnp.sum(jnp.where(lane == h + A_HEADS, ba, 0.0), axis=-1, keepdims=True)
    beta = jnp.broadcast_to(_sigmoid(bl), (T, LANES))
    a_log = hp_ref[hh, 0:1, :]
    dt_b = hp_ref[hh, 1:2, :]
    z = jnp.broadcast_to(al, (T, LANES)) + dt_b
    softplus = jnp.maximum(z, 0.0) + jnp.log(1.0 + jnp.exp(-jnp.abs(z)))
    g = -jnp.exp(a_log) * softplus

    _, _, lower, strict, same = _block_masks(T)
    in_chunk = same(A_CHUNK)
    tri = jnp.where(lower & in_chunk, 1.0, 0.0)
    ones_bd = jnp.where(in_chunk, 1.0, 0.0)
    gc = _dot_f32(tri, g)
    glast = _dot_f32(ones_bd, g)
    gc2 = jnp.concatenate([gc, gc], axis=1)
    diff = gc2 - gc2.T
    lo_mask = lower & in_chunk
    decay = jnp.where(lo_mask, jnp.exp(jnp.where(lo_mask, diff, 0.0)), 0.0)

    kb = k * beta
    a_mat = jnp.where(strict & in_chunk, _dot_nt(kb, k) * decay, 0.0)
    qk = jnp.where(lo_mask, _dot_nt(q, k) * decay, 0.0)
    eg = jnp.exp(gc)
    rhs = jnp.concatenate([v * beta, kb * eg], axis=1)

    s16, s32 = same(16), same(32)
    eye = jnp.where(lower & ~strict, 1.0, 0.0)
    d = jnp.where(s16, a_mat, 0.0)
    d2 = _dot(d, d)
    d4 = _dot(d2, d2)
    d8 = _dot(d4, d4)
    p = eye - d
    p = p + _dot(p, d2)
    p = p + _dot(p, d4)
    p = p + _dot(p, d8)
    e1 = jnp.where(s32 & ~s16, a_mat, 0.0)
    p = p - _dot(_dot(p, e1), p)
    e2 = jnp.where(in_chunk & ~s32, a_mat, 0.0)
    p = p - _dot(_dot(p, e2), p)
    wy = _dot(p, rhs)
    u = wy[:, :LANES]
    k_cum = wy[:, LANES:]

    q_dec = q * eg
    k_dec = k * jnp.exp(glast - gc)
    cd = jnp.exp(glast)

    gon = gon_ref[...]
    gate = gate_ref[0, :, ls].astype(F32)
    state = state_ref[hh]
    for c in range(GDN_NC):
        r = slice(c * A_CHUNK, (c + 1) * A_CHUNK)
        kq = jnp.concatenate([k_cum[r], q_dec[r]], axis=0)
        res = _dot(kq, state)
        v_new = u[r] - res[:A_CHUNK]
        o_c = res[A_CHUNK:] + _dot(qk[r, c * A_CHUNK:(c + 1) * A_CHUNK], v_new)
        state = state * cd[c * A_CHUNK:c * A_CHUNK + 1, :] + _dot_tn(k_dec[r], v_new)
        ms = jnp.mean(o_c * o_c, axis=-1, keepdims=True)
        o_n = o_c * lax.rsqrt(ms + NORM_EPS) * gon
        o_ref[0, r, ls] = (o_n * _silu(gate[r])).astype(o_ref.dtype)
    state_ref[hh] = state


def _gdn(proj3, ba3, cw, hp, gon):
    bsz, seq, _ = proj3.shape
    T = GDN_T
    hb = GDN_HB
    wblk = hb * LANES
    qb, kb_, vb, gb = (COL_A_QKV // wblk, COL_A_QKV // wblk + A_HEADS // hb,
                       COL_A_QKV // wblk + 2 * A_HEADS // hb, COL_A_GATE // wblk)
    return pl.pallas_call(
        _gdn_kernel,
        grid=(bsz, A_HEADS // hb, seq // T),
        in_specs=[
            pl.BlockSpec((1, T, wblk), lambda b, h, t: (b, t, qb + h)),
            pl.BlockSpec((1, T, wblk), lambda b, h, t: (b, t, kb_ + h)),
            pl.BlockSpec((1, T, wblk), lambda b, h, t: (b, t, vb + h)),
            pl.BlockSpec((1, T, wblk), lambda b, h, t: (b, t, gb + h)),
            pl.BlockSpec((1, T, LANES), lambda b, h, t: (b, t, 0)),
            pl.BlockSpec((hb, 3, 8, LANES), lambda b, h, t: (h, 0, 0, 0)),
            pl.BlockSpec((hb, 8, LANES), lambda b, h, t: (h, 0, 0)),
            pl.BlockSpec((1, LANES), lambda b, h, t: (0, 0)),
        ],
        out_specs=pl.BlockSpec((1, T, wblk), lambda b, h, t: (b, t, h)),
        out_shape=jax.ShapeDtypeStruct((bsz, seq, A_HEADS * A_DK), MXU_DTYPE),
        scratch_shapes=[
            pltpu.VMEM((hb, A_DK, A_DK), F32),
            pltpu.VMEM((3 * hb, T + 8, LANES), F32),
        ],
        compiler_params=_cparams(("arbitrary", "arbitrary", "arbitrary")),
        name="gdn",
    )(proj3, proj3, proj3, proj3, ba3, cw, hp, gon)


def _attn_kernel(q_ref, kp_ref, kc_ref, vp_ref, vc_ref, o_ref, lse_ref, *, dilation, group):
    n = pl.program_id(2)
    blk = B_BLOCK
    i = lax.broadcasted_iota(jnp.int32, (blk, 2 * blk), 0)
    j = lax.broadcasted_iota(jnp.int32, (blk, 2 * blk), 1)
    rel = i + blk - j
    valid = (rel >= 0) & (rel <= blk) & jnp.logical_or(j >= blk, n > 0)
    relf = rel.astype(F32)
    neg = jnp.where(valid, 0.0, -jnp.inf)
    scale = B_HD ** -0.5
    for hh in range(B_HPG):
        head = group * B_HPG + hh
        slope = 2.0 ** (-ALIBI_MAX_BIAS * (head + 1) / B_HEADS)
        sl = slice(hh * B_HD, (hh + 1) * B_HD)
        kk = jnp.concatenate([kp_ref[0, :, sl], kc_ref[0, :, sl]], axis=0)
        vv = jnp.concatenate([vp_ref[0, :, sl], vc_ref[0, :, sl]], axis=0)
        s = _dot_nt(q_ref[0, :, sl], kk) * scale + relf * (-slope * dilation) + neg
        m = jnp.max(s, axis=-1, keepdims=True)
        p = jnp.exp(s - m)
        l = jnp.sum(p, axis=-1, keepdims=True)
        acc = _dot(p, vv)
        o_ref[0, :, sl] = acc / l
        lse_ref[0, :, sl] = jnp.broadcast_to(m + jnp.log(l), (blk, B_HD))


def _attn_group(arr3, group, dilation, qcol):
    bsz, seq, width = arr3.shape
    rows = seq // dilation
    nb = rows // B_BLOCK
    wblk = B_HPG * B_HD
    per_row = width // wblk
    view = arr3.reshape(bsz, rows, dilation * width)
    qc = qcol // wblk
    kc = qc + 1
    vc = kc + 1
    cur = lambda c: (lambda b, r, n: (b, n, r * per_row + c))
    prev = lambda c: (lambda b, r, n: (b, jnp.maximum(n - 1, 0), r * per_row + c))
    blk = (1, B_BLOCK, wblk)
    o, lse = pl.pallas_call(
        functools.partial(_attn_kernel, dilation=dilation, group=group),
        grid=(bsz, dilation, nb),
        in_specs=[
            pl.BlockSpec(blk, cur(qc)),
            pl.BlockSpec(blk, prev(kc)),
            pl.BlockSpec(blk, cur(kc)),
            pl.BlockSpec(blk, prev(vc)),
            pl.BlockSpec(blk, cur(vc)),
        ],
        out_specs=[
            pl.BlockSpec(blk, lambda b, r, n: (b, n, r)),
            pl.BlockSpec(blk, lambda b, r, n: (b, n, r)),
        ],
        out_shape=[
            jax.ShapeDtypeStruct((bsz, rows, dilation * wblk), F32),
            jax.ShapeDtypeStruct((bsz, rows, dilation * wblk), F32),
        ],
        compiler_params=_cparams(("arbitrary", "arbitrary", "arbitrary")),
        name=f"attn_g{group}",
    )(view, view, view, view, view)
    return o.reshape(bsz, seq, wblk), lse.reshape(bsz, seq, wblk)


def _attn_combine_kernel(o0, l0, o1, l1, o2, l2, gate_ref, out_ref):
    a0, a1, a2 = l0[0], l1[0], l2[0]
    m = jnp.maximum(jnp.maximum(a0, a1), a2)
    w0, w1, w2 = jnp.exp(a0 - m), jnp.exp(a1 - m), jnp.exp(a2 - m)
    o = (w0 * o0[0] + w1 * o1[0] + w2 * o2[0]) / (w0 + w1 + w2)
    out_ref[0] = (o * _silu(gate_ref[0].astype(F32))).astype(out_ref.dtype)


def _attn_combine(parts, proj3):
    bsz, seq, _ = proj3.shape
    wblk = B_HPG * B_HD
    ts = 512
    spec = pl.BlockSpec((1, ts, wblk), lambda b, t: (b, t, 0))
    return pl.pallas_call(
        _attn_combine_kernel,
        grid=(bsz, seq // ts),
        in_specs=[spec] * 6 + [pl.BlockSpec((1, ts, wblk), lambda b, t: (b, t, COL_B_GATE // wblk))],
        out_specs=spec,
        out_shape=jax.ShapeDtypeStruct((bsz, seq, wblk), MXU_DTYPE),
        compiler_params=_cparams(("arbitrary", "arbitrary")),
        name="attn_combine",
    )(*parts, proj3)


GMLP_T = 256


def _gmlp_kernel(u_ref, v_ref, gate_ref, lng_ref, lnb_ref, ws_ref, bias_ref, o_ref):
    width = C_GROUPS * LANES
    vv = _gelu_tanh(v_ref[...].astype(F32))
    mu = jnp.mean(vv, axis=-1, keepdims=True)
    xc = vv - mu
    var = jnp.mean(xc * xc, axis=-1, keepdims=True)
    vn = xc * lax.rsqrt(var + NORM_EPS) * lng_ref[...] + lnb_ref[...]
    ti = lax.broadcasted_iota(jnp.int32, (C_CHUNK, C_CHUNK), 0)
    si = lax.broadcasted_iota(jnp.int32, (C_CHUNK, C_CHUNK), 1)
    causal = ti >= si
    for g in range(C_GROUPS):
        w = jnp.where(causal, ws_ref[g], 0.0)
        cs = slice(g * LANES, (g + 1) * LANES)
        for c in range(GMLP_T // C_CHUNK):
            rs = slice(c * C_CHUNK, (c + 1) * C_CHUNK)
            sv = _dot(w, vn[rs, cs]) + bias_ref[:, cs]
            uu = _gelu_tanh(u_ref[rs, cs].astype(F32))
            o_ref[rs, cs] = (uu * sv * _silu(gate_ref[rs, cs].astype(F32))).astype(o_ref.dtype)


def _gmlp(proj2, ln_g, ln_b, w_s, bias_full):
    m, _ = proj2.shape
    width = C_GROUPS * LANES
    T = GMLP_T
    col = lambda c: (lambda i: (i, c // width))
    return pl.pallas_call(
        _gmlp_kernel,
        grid=(m // T,),
        in_specs=[
            pl.BlockSpec((T, width), col(COL_C_U)),
            pl.BlockSpec((T, width), col(COL_C_V)),
            pl.BlockSpec((T, width), col(COL_C_GATE)),
            pl.BlockSpec((1, width), lambda i: (0, 0)),
            pl.BlockSpec((1, width), lambda i: (0, 0)),
            pl.BlockSpec((C_GROUPS, C_CHUNK, C_CHUNK), lambda i: (0, 0, 0)),
            pl.BlockSpec((C_CHUNK, width), lambda i: (0, 0)),
        ],
        out_specs=pl.BlockSpec((T, width), lambda i: (i, 0)),
        out_shape=jax.ShapeDtypeStruct((m, width), MXU_DTYPE),
        compiler_params=_cparams(("arbitrary",)),
        name="gmlp",
    )(proj2, proj2, proj2, ln_g.reshape(1, width), ln_b.reshape(1, width), w_s, bias_full)


def _merge_kernel(a_ref, b_ref, c_ref, wa_ref, wb_ref, wc_ref, ga_ref, gb_ref, gc_ref, o_ref):
    sg = lambda r: _sigmoid(r[...].astype(F32))
    y = sg(ga_ref) * jnp.dot(a_ref[...], wa_ref[...], preferred_element_type=F32)
    y += sg(gb_ref) * jnp.dot(b_ref[...], wb_ref[...], preferred_element_type=F32)
    y += sg(gc_ref) * jnp.dot(c_ref[...], wc_ref[...], preferred_element_type=F32)
    o_ref[...] = y.astype(o_ref.dtype)


def _merge(ya, yb, yc, wa, wb, wc, proj2):
    m = ya.shape[0]
    n = wa.shape[1]
    tm, tn = 512, 512
    gcol = lambda br: (lambda i, j: (i, (COL_MERGE + br * n) // tn + j))
    act = lambda k: pl.BlockSpec((tm, k), lambda i, j: (i, 0))
    wsp = lambda k: pl.BlockSpec((k, tn), lambda i, j: (0, j))
    return pl.pallas_call(
        _merge_kernel,
        grid=(m // tm, n // tn),
        in_specs=[
            act(ya.shape[1]), act(yb.shape[1]), act(yc.shape[1]),
            wsp(wa.shape[0]), wsp(wb.shape[0]), wsp(wc.shape[0]),
            pl.BlockSpec((tm, tn), gcol(0)),
            pl.BlockSpec((tm, tn), gcol(1)),
            pl.BlockSpec((tm, tn), gcol(2)),
        ],
        out_specs=pl.BlockSpec((tm, tn), lambda i, j: (i, j)),
        out_shape=jax.ShapeDtypeStruct((m, n), MXU_DTYPE),
        compiler_params=_cparams(("arbitrary", "arbitrary")),
        name="merge",
    )(ya, yb, yc, wa, wb, wc, proj2, proj2, proj2)


def _post_kernel(x_ref, o_ref, g_ref, gate_ref, y_ref):
    o = o_ref[0]
    ms = jnp.mean(o * o, axis=-1, keepdims=True)
    y_ref[0] = x_ref[0] + gate_ref[0] * (o * lax.rsqrt(ms + NORM_EPS) * g_ref[...])


def _post(x, out3, g_post, mod3):
    bsz, seq, d = x.shape
    ts = 512
    spec = pl.BlockSpec((1, ts, d), lambda b, t: (b, t, 0))
    return pl.pallas_call(
        _post_kernel,
        grid=(bsz, seq // ts),
        in_specs=[
            spec, spec,
            pl.BlockSpec((1, d), lambda b, t: (0, 0)),
            pl.BlockSpec((1, 1, d), lambda b, t: (b, 0, 2)),
        ],
        out_specs=spec,
        out_shape=jax.ShapeDtypeStruct((bsz, seq, d), F32),
        compiler_params=_cparams(("arbitrary", "arbitrary")),
        name="post",
    )(x, out3, g_post.reshape(1, d), mod3)


def _pack_w_in(w_in_l):
    n_qkv = 3 * A_HEADS * A_DK
    ba0 = n_qkv
    ag0 = ba0 + 2 * A_HEADS
    bq0 = ag0 + A_HEADS * A_DK
    hw = B_HEADS * B_HD
    gw = B_HPG * B_HD
    rest0 = bq0 + 3 * hw
    qkv = lambda gi: [w_in_l[:, bq0 + p * hw + gi * gw: bq0 + p * hw + (gi + 1) * gw] for p in range(3)]
    w_main = jnp.concatenate([w_in_l[:, :n_qkv], w_in_l[:, ag0:bq0]] + qkv(0) + [w_in_l[:, rest0:]], axis=1)
    w_ba = jnp.pad(w_in_l[:, ba0:ag0], ((0, 0), (0, LANES - 2 * A_HEADS)))
    w_groups = [jnp.concatenate(qkv(gi), axis=1).astype(MXU_DTYPE) for gi in range(1, len(B_GROUPS))]
    return w_main.astype(MXU_DTYPE), w_ba.astype(MXU_DTYPE), w_groups


def _layer(x, mod_l, g_pre, g_post, w_in_l, conv_a, a_log, dt_bias, g_onorm_a,
           ln_c_g, ln_c_b, w_spatial, b_spatial, w_br_a, w_br_b, w_br_c, w_out):
    bsz, seq, d = x.shape
    m = bsz * seq
    mod3 = mod_l[:bsz].reshape(bsz, 1, 3 * d)
    w_main, w_ba, w_groups = _pack_w_in(w_in_l)

    h, ba = _prenorm(x, g_pre, mod3, w_ba)
    h2 = h.reshape(m, d)
    proj = _matmul(h2, w_main, MXU_DTYPE, 1024, 1024, "in_proj")
    proj3 = proj.reshape(bsz, seq, MAIN_WIDTH)
    proj_groups = [_matmul(h2, w, MXU_DTYPE, 1024, 1024, f"in_proj_g{gi + 1}").reshape(bsz, seq, -1)
                   for gi, w in enumerate(w_groups)]

    cw = conv_a.reshape(A_CONV, 3, A_HEADS, A_DK).transpose(2, 1, 0, 3)
    cw = jnp.pad(cw, ((0, 0), (0, 0), (0, 8 - A_CONV), (0, 0)))
    hp = jnp.stack([a_log, dt_bias], axis=1)
    hp = jnp.broadcast_to(jnp.pad(hp, ((0, 0), (0, 6)))[:, :, None], (A_HEADS, 8, LANES))
    ya = _gdn(proj3, ba, cw, hp, g_onorm_a.reshape(1, A_DK))

    parts = []
    for gi, (_, dilation) in enumerate(B_GROUPS):
        if gi == 0:
            parts.extend(_attn_group(proj3, gi, dilation, COL_B_QKV))
        else:
            parts.extend(_attn_group(proj_groups[gi - 1], gi, dilation, 0))
    yb = _attn_combine(parts, proj3)

    bias_full = jnp.repeat(b_spatial.T, LANES, axis=1)
    yc = _gmlp(proj, ln_c_g, ln_c_b, w_spatial, bias_full)

    y = _merge(ya.reshape(m, -1), yb.reshape(m, -1), yc,
               w_br_a.astype(MXU_DTYPE), w_br_b.astype(MXU_DTYPE), w_br_c.astype(MXU_DTYPE), proj)
    out = _matmul(y, w_out.astype(MXU_DTYPE), F32, 1024, 1024, "out_proj")
    return _post(x, out.reshape(bsz, seq, d), g_post, mod3)


def kernel(x, c, w_ada, b_ada, g_pre, g_post, w_in, conv_a, a_log, dt_bias, g_onorm_a,
           ln_c_g, ln_c_b, w_spatial, b_spatial, w_br_a, w_br_b, w_br_c, w_out):
    bsz = x.shape[0]
    depth = w_ada.shape[0]
    c_pad = jnp.pad(c, ((0, 8 - bsz), (0, 0)))
    mod = _ada(c_pad, w_ada, b_ada)
    for l in range(depth):
        x = _layer(x, mod[l], g_pre[l], g_post[l], w_in[l], conv_a[l], a_log[l], dt_bias[l],
                   g_onorm_a[l], ln_c_g[l], ln_c_b[l], w_spatial[l], b_spatial[l],
                   w_br_a[l], w_br_b[l], w_br_c[l], w_out[l])
    return x
```
